```python
import math
import jax
import jax.numpy as jnp
from jax import lax
import numpy as np

D_MODEL = 4096
BATCH = 16
SEQ = 256
DEPTH = 2
DEC_BATCH = 4
DEC_SEQ = 2048
PAST_LEN = 256

GRID_W = 64
N_GROUPS = 4
GROUP_W = D_MODEL // N_GROUPS
MIX_W = N_GROUPS * GROUP_W
FT_BLOCKS = 4
FT_CH = GROUP_W // FT_BLOCKS
NA_HEADS = 16
NA_HD = GROUP_W // NA_HEADS
NA_KR_MAX = 8
NA_KC = 16
Q_BLOCK = 128
RW_N = 64
RW_HEADS = GROUP_W // RW_N
RW_DECAY_LORA = 64
RW_A_LORA = 64
RW_G_LORA = 160
RW_COLS = 3 * GROUP_W + 2 * RW_DECAY_LORA + 2 * RW_A_LORA + RW_G_LORA
RW_DECAY_SCALE = math.exp(-0.5)
RW_GN_EPS = 64e-5
S5_IN = 16
S5_GROUPS = GROUP_W // S5_IN
S5_N = 64
S5_DT_MIN = 1e-3
S5_DT_MAX = 1e-1
N_EXPERTS = 16
EXPERT_FF = D_MODEL // 2
EC_CAPACITY = 2
LN_EPS = 1e-5
DN_ALPHA = (2 * DEPTH) ** 0.25
DN_BETA = (8 * DEPTH) ** -0.25
OFF_FT = 0
OFF_NA = OFF_FT + GROUP_W
OFF_RW = OFF_NA + 3 * GROUP_W
OFF_S5 = OFF_RW + RW_COLS
IN_COLS = OFF_S5 + GROUP_W

kernel_name = 'hybrid_diffusion_prefix_step'


def _f32(t):
    return t.astype(jnp.float32)


def layer_norm(x, g=None, b=None):
    xf = _f32(x)
    mu = jnp.mean(xf, -1, keepdims=True)
    var = jnp.mean(jnp.square(xf - mu), -1, keepdims=True)
    y = (xf - mu) * lax.rsqrt(var + LN_EPS)
    if g is not None:
        y = y * _f32(g) + _f32(b)
    return y.astype(x.dtype)


def adaln_mods(cond, w_ada, b_ada):
    m = jax.nn.silu(cond) @ w_ada + b_ada
    return jnp.split(m, 6, axis=-1)


def modulate(x, shift, scale):
    return layer_norm(x) * (1 + scale) + shift


def centred_conv3(x, w):
    xp = jnp.pad(x, ((0, 0), (1, 1), (0, 0)))
    return w[0] * xp[:, :-2] + w[1] * xp[:, 1:-1] + w[2] * xp[:, 2:]


def fourier_mix(u, ft_w):
    B, T, _ = u.shape
    ub = _f32(u).reshape(B, T, FT_BLOCKS, FT_CH)
    f = jnp.fft.fftn(ub, axes=(1, 3), norm='ortho').real
    return f.reshape(B, T, GROUP_W).astype(u.dtype) @ ft_w


def context_attention(q, k, v):
    B, L, H, hd = q.shape
    nb = L // Q_BLOCK
    qb = jnp.moveaxis(q.reshape(B, nb, Q_BLOCK, H, hd), 1, 0)
    scale = hd ** -0.5

    def block(q_blk):
        s = _f32(jnp.einsum('bqhd,bkhd->bhqk', q_blk, k)) * scale
        p = jax.nn.softmax(s, -1).astype(v.dtype)
        return jnp.einsum('bhqk,bkhd->bqhd', p, v)

    o = lax.map(block, qb)
    return jnp.moveaxis(o, 0, 1).reshape(B, L, H, hd)


def neighbourhood_attention(q, k, v, k_ctx, v_ctx, rpb):
    B, T, H, hd = q.shape
    rows = T // GRID_W
    kr = min(NA_KR_MAX, rows)
    n_loc = kr * NA_KC
    scale = hd ** -0.5
    qg = q.reshape(B, rows, GRID_W, H, hd)
    kg = k.reshape(B, rows, GRID_W, H, hd)
    vg = v.reshape(B, rows, GRID_W, H, hd)
    cols = jnp.arange(GRID_W)
    col_idx = jnp.clip(cols - NA_KC // 2, 0, GRID_W - NA_KC)[:, None] + jnp.arange(NA_KC)
    dc = col_idx - cols[:, None] + NA_KC - 1

    def one_row(r):
        rs = jnp.clip(r - kr // 2, 0, rows - kr)
        q_r = lax.dynamic_index_in_dim(qg, r, axis=1, keepdims=False)
        k_blk = lax.dynamic_slice_in_dim(kg, rs, kr, axis=1)[:, :, col_idx]
        v_blk = lax.dynamic_slice_in_dim(vg, rs, kr, axis=1)[:, :, col_idx]
        dr = rs + jnp.arange(kr) - r + NA_KR_MAX - 1
        bias = jnp.transpose(rpb[:, dr[:, None, None], dc[None]], (0, 2, 1, 3))
        s_loc = _f32(jnp.einsum('bqhd,brqkhd->bhqrk', q_r, k_blk)) * scale + _f32(bias)[None]
        s_ctx = _f32(jnp.einsum('bqhd,blhd->bhql', q_r, k_ctx)) * scale
        s = jnp.concatenate([s_loc.reshape(B, H, GRID_W, n_loc), s_ctx], -1)
        p = jax.nn.softmax(s, -1).astype(v.dtype)
        p_loc = p[..., :n_loc].reshape(B, H, GRID_W, kr, NA_KC)
        return (jnp.einsum('bhqrk,brqkhd->bqhd', p_loc, v_blk)
                + jnp.einsum('bhql,blhd->bqhd', p[..., n_loc:], v_ctx))

    o = lax.map(one_row, jnp.arange(rows))
    return jnp.moveaxis(o, 0, 1).reshape(B, T, H, hd)


def wkv7_scan(s0, r, w, k, v, kk, a, reverse):
    def step(S, inp):
        r_t, w_t, k_t, v_t, kk_t, a_t = inp
        sa = jnp.einsum('bhij,bhj->bhi', S, kk_t)
        S = (S * w_t[:, :, None, :] - sa[..., None] * (kk_t * a_t)[:, :, None, :]
             + v_t[..., None] * k_t[:, :, None, :])
        return S, jnp.einsum('bhij,bhj->bhi', S, r_t)

    seq = tuple(jnp.swapaxes(t, 0, 1) for t in (r, w, k, v, kk, a))
    s_fin, y = lax.scan(step, s0, seq, reverse=reverse)
    return s_fin, jnp.swapaxes(y, 0, 1)


def rwkv7_mixer(z, lp, s0):
    B, T, _ = z.shape
    G = GROUP_W
    zf = _f32(centred_conv3(z, lp['rw_conv']))
    r, k, v = zf[..., :G], zf[..., G:2 * G], zf[..., 2 * G:3 * G]
    o1 = 3 * G + 2 * RW_DECAY_LORA
    o2 = o1 + 2 * RW_A_LORA
    wd = zf[..., 3 * G:o1].reshape(B, T, 2, RW_DECAY_LORA)
    ad = zf[..., o1:o2].reshape(B, T, 2, RW_A_LORA)
    gd = zf[..., o2:]
    d_log = _f32(lp['rw_w0']) + jnp.einsum('btzl,zlc->btzc', jnp.tanh(wd), _f32(lp['rw_w_up']))
    decay = jnp.exp(-RW_DECAY_SCALE * jax.nn.sigmoid(d_log))
    a = jax.nn.sigmoid(_f32(lp['rw_a0']) + jnp.einsum('btzl,zlc->btzc', ad, _f32(lp['rw_a_up'])))
    g = jax.nn.sigmoid(gd) @ _f32(lp['rw_g_up'])

    def heads(t):
        return t.reshape(t.shape[:-1] + (RW_HEADS, RW_N))

    kk = heads(k * _f32(lp['rw_k_k']))
    kk = kk / jnp.maximum(jnp.sqrt(jnp.sum(kk * kk, -1, keepdims=True)), 1e-12)
    k_dir = heads(k[:, :, None, :] * (1 + (a - 1) * _f32(lp['rw_k_a'])))
    decay_h, a_h = heads(decay), heads(a)
    rh, vh = heads(r), heads(v)
    ys, finals = [], []
    for d, rev in ((0, False), (1, True)):
        s_fin, y_d = wkv7_scan(s0[:, d], rh, decay_h[:, :, d], k_dir[:, :, d], vh, kk, a_h[:, :, d], rev)
        ys.append(y_d)
        finals.append(s_fin)
    y = ys[0] + ys[1]
    mu = jnp.mean(y, -1, keepdims=True)
    var = jnp.mean(jnp.square(y - mu), -1, keepdims=True)
    yn = ((y - mu) * lax.rsqrt(var + RW_GN_EPS)).reshape(B, T, G) * _f32(lp['rw_gn_w']) + _f32(lp['rw_gn_b'])
    bonus = jnp.einsum('bthn,btzhn,hn->bth', rh, k_dir, _f32(lp['rw_r_k']))[..., None] * vh
    out = (yn + bonus.reshape(B, T, G)) * g
    return out.astype(z.dtype), jnp.stack(finals, 1)


def diag_scan(lam_bar, bu, h0, reverse):
    edge = -1 if reverse else 0
    bu = bu.at[:, edge].add(lam_bar * h0)
    a = jnp.broadcast_to(lam_bar, bu.shape)

    def combine(e1, e2):
        a1, b1 = e1
        a2, b2 = e2
        return a1 * a2, a2 * b1 + b2

    _, h = lax.associative_scan(combine, (a, bu), reverse=reverse, axis=1)
    return h


def s5_mixer(z, lp, h0):
    B, T, _ = z.shape
    u = _f32(z).reshape(B, T, S5_GROUPS, S5_IN)
    uc = u.astype(jnp.complex64)
    y = u * _f32(lp['s5_d'])
    finals = []
    for d, rev in ((0, False), (1, True)):
        lam = lax.complex(_f32(lp['s5_a_re'][d]), _f32(lp['s5_a_im'][d]))
        dt = jnp.exp(_f32(lp['s5_log_dt'][d]))[:, None]
        lam_bar = jnp.exp(lam * dt)
        b_mat = lax.complex(_f32(lp['s5_b_re'][d]), _f32(lp['s5_b_im'][d]))
        b_bar = ((lam_bar - 1) / lam)[..., None] * b_mat
        bu = jnp.einsum('gpi,btgi->btgp', b_bar, uc)
        h = diag_scan(lam_bar, bu, h0[:, d], rev)
        c_mat = lax.complex(_f32(lp['s5_c_re'][d]), _f32(lp['s5_c_im'][d]))
        y = y + jnp.einsum('gip,btgp->btgi', c_mat, h).real
        finals.append(h[:, 0] if rev else h[:, -1])
    yg = jax.nn.gelu(y.reshape(B, T, GROUP_W))
    out = yg * jax.nn.sigmoid(yg @ _f32(lp['s5_w_glu']))
    return out.astype(z.dtype), jnp.stack(finals, 1)


def mixing(u, lp, ctx):
    B, T, _ = u.shape
    z = u @ lp['w_in']
    y_ft = fourier_mix(z[..., OFF_FT:OFF_NA], lp['ft_w'])
    qkv = z[..., OFF_NA:OFF_RW].reshape(B, T, 3, NA_HEADS, NA_HD)
    q, k, v = qkv[:, :, 0], qkv[:, :, 1], qkv[:, :, 2]
    if ctx is None:
        y_na = context_attention(q, k, v)
        rw0 = jnp.zeros((B, 2, RW_HEADS, RW_N, RW_N), jnp.float32)
        s50 = jnp.zeros((B, 2, S5_GROUPS, S5_N), jnp.complex64)
    else:
        k_ctx, v_ctx, rw_state, s5_state = ctx
        y_na = neighbourhood_attention(q, k, v, k_ctx, v_ctx, lp['na_rpb'])
        rw0 = _f32(rw_state)
        s50 = lax.complex(_f32(s5_state[..., 0]), _f32(s5_state[..., 1]))
    y_rw, rw_fin = rwkv7_mixer(z[..., OFF_RW:OFF_S5], lp, rw0)
    y_s5, s5_fin = s5_mixer(z[..., OFF_S5:IN_COLS], lp, s50)
    y = jnp.concatenate([y_ft, y_na.reshape(B, T, GROUP_W), y_rw, y_s5], -1) @ lp['w_out']
    if ctx is None:
        s5_real = jnp.stack([s5_fin.real, s5_fin.imag], -1).astype(u.dtype)
        return y, (k, v, rw_fin.astype(u.dtype), s5_real)
    return y, None


def expert_choice_ffn(u, w_router, w1, w3, w2):
    B, T, D = u.shape
    n = B * T
    cap = EC_CAPACITY * n // N_EXPERTS
    xf = u.reshape(n, D)
    aff = jax.nn.softmax(_f32(xf @ w_router), -1)
    gate, idx = lax.top_k(aff.T, cap)
    xs = xf[idx]
    h = jax.nn.silu(jnp.einsum('ecd,edf->ecf', xs, w1)) * jnp.einsum('ecd,edf->ecf', xs, w3)
    ye = jnp.einsum('ecf,efd->ecd', h, w2) * gate[..., None].astype(u.dtype)
    out = jnp.zeros_like(xf).at[idx.reshape(-1)].add(ye.reshape(-1, D))
    return out.reshape(B, T, D)


def trunk_layer(x, cond, lp, ctx):
    sh1, sc1, g1, sh2, sc2, g2 = adaln_mods(cond, lp['w_ada'], lp['b_ada'])
    m, ctx_out = mixing(modulate(x, sh1, sc1), lp, ctx)
    x = layer_norm(DN_ALPHA * x + g1 * m, lp['ln1_g'], lp['ln1_b'])
    f = expert_choice_ffn(modulate(x, sh2, sc2), lp['moe_router'], lp['moe_w1'], lp['moe_w3'], lp['moe_w2'])
    x = layer_norm(DN_ALPHA * x + g2 * f, lp['ln2_g'], lp['ln2_b'])
    return x, ctx_out


def setup_inputs(seed: int = 0) -> dict:
    key = jax.random.key(seed)
    keys = iter(jax.random.split(key, 64))

    def nrm(shape, std):
        return std * jax.random.normal(next(keys), shape, jnp.float32)

    D, G, L = D_MODEL, GROUP_W, DEPTH
    conv_centre = jnp.array([0.25, 0.5, 0.25], jnp.float32)[None, :, None]
    s5_im = math.pi * jnp.arange(S5_N, dtype=jnp.float32)
    return {
        'x_prompt': nrm((BATCH, SEQ, D), 1.0),
        'x_sample': nrm((DEC_BATCH, DEC_SEQ, D), 1.0),
        'c': nrm((DEC_BATCH, D), 1.0),
        'cache_na_k': nrm((DEC_BATCH, L, PAST_LEN, NA_HEADS, NA_HD), 1.0),
        'cache_na_v': nrm((DEC_BATCH, L, PAST_LEN, NA_HEADS, NA_HD), 1.0),
        'state_rwkv': nrm((DEC_BATCH, L, 2, RW_HEADS, RW_N, RW_N), 0.3),
        'state_s5': nrm((DEC_BATCH, L, 2, S5_GROUPS, S5_N, 2), 0.3),
        'c_ctx': nrm((D,), 1.0),
        'w_ada': nrm((L, D, 6 * D), 0.5 * D ** -0.5),
        'b_ada': nrm((L, 6 * D), 0.02),
        'w_in': nrm((L, D, IN_COLS), D ** -0.5),
        'w_out': nrm((L, MIX_W, D), DN_BETA * MIX_W ** -0.5),
        'ln1_g': 1.0 + nrm((L, D), 0.02),
        'ln1_b': nrm((L, D), 0.02),
        'ln2_g': 1.0 + nrm((L, D), 0.02),
        'ln2_b': nrm((L, D), 0.02),
        'ft_w': nrm((L, G, G), G ** -0.5),
        'na_rpb': nrm((L, NA_HEADS, 2 * NA_KR_MAX - 1, 2 * NA_KC - 1), 0.02),
        'rw_conv': conv_centre + nrm((L, 3, RW_COLS), 0.05),
        'rw_w_up': nrm((L, 2, RW_DECAY_LORA, G), 0.1),
        'rw_w0': nrm((L, 2, G), 1.0),
        'rw_a_up': nrm((L, 2, RW_A_LORA, G), 0.1),
        'rw_a0': nrm((L, 2, G), 0.5),
        'rw_g_up': nrm((L, RW_G_LORA, G), RW_G_LORA ** -0.5),
        'rw_k_k': 0.85 + nrm((L, G), 0.02),
        'rw_k_a': 1.0 + nrm((L, G), 0.02),
        'rw_r_k': nrm((L, RW_HEADS, RW_N), 0.1),
        'rw_gn_w': 1.0 + nrm((L, G), 0.02),
        'rw_gn_b': nrm((L, G), 0.02),
        's5_a_re': -0.5 + nrm((L, 2, S5_GROUPS, S5_N), 0.01),
        's5_a_im': s5_im + nrm((L, 2, S5_GROUPS, S5_N), 0.01),
        's5_log_dt': jax.random.uniform(next(keys), (L, 2, S5_GROUPS), jnp.float32,
                                        math.log(S5_DT_MIN), math.log(S5_DT_MAX)),
        's5_b_re': nrm((L, 2, S5_GROUPS, S5_N, S5_IN), (2 * S5_IN) ** -0.5),
        's5_b_im': nrm((L, 2, S5_GROUPS, S5_N, S5_IN), (2 * S5_IN) ** -0.5),
        's5_c_re': nrm((L, 2, S5_GROUPS, S5_IN, S5_N), S5_N ** -0.5),
        's5_c_im': nrm((L, 2, S5_GROUPS, S5_IN, S5_N), S5_N ** -0.5),
        's5_d': nrm((L, S5_GROUPS, S5_IN), 0.5),
        's5_w_glu': nrm((L, G, G), G ** -0.5),
        'moe_router': nrm((L, D, N_EXPERTS), D ** -0.5),
        'moe_w1': nrm((L, N_EXPERTS, D, EXPERT_FF), D ** -0.5),
        'moe_w3': nrm((L, N_EXPERTS, D, EXPERT_FF), D ** -0.5),
        'moe_w2': nrm((L, N_EXPERTS, EXPERT_FF, D), DN_BETA * EXPERT_FF ** -0.5),
    }


def reference(x_prompt, x_sample, c, cache_na_k, cache_na_v, state_rwkv, state_s5, c_ctx,
              w_ada, b_ada, w_in, w_out, ln1_g, ln1_b, ln2_g, ln2_b, ft_w, na_rpb,
              rw_conv, rw_w_up, rw_w0, rw_a_up, rw_a0, rw_g_up, rw_k_k, rw_k_a, rw_r_k,
              rw_gn_w, rw_gn_b, s5_a_re, s5_a_im, s5_log_dt, s5_b_re, s5_b_im, s5_c_re,
              s5_c_im, s5_d, s5_w_glu, moe_router, moe_w1, moe_w3, moe_w2):
    y_p, y_s = x_prompt, x_sample
    cond_ctx = c_ctx[None, None, :]
    cond_lat = c[:, None, :]
    ks, vs, rws, s5s = [], [], [], []
    for l in range(DEPTH):
        lp = {
            'w_ada': w_ada[l], 'b_ada': b_ada[l], 'w_in': w_in[l], 'w_out': w_out[l],
            'ln1_g': ln1_g[l], 'ln1_b': ln1_b[l], 'ln2_g': ln2_g[l], 'ln2_b': ln2_b[l],
            'ft_w': ft_w[l], 'na_rpb': na_rpb[l],
            'rw_conv': rw_conv[l], 'rw_w_up': rw_w_up[l], 'rw_w0': rw_w0[l],
            'rw_a_up': rw_a_up[l], 'rw_a0': rw_a0[l], 'rw_g_up': rw_g_up[l],
            'rw_k_k': rw_k_k[l], 'rw_k_a': rw_k_a[l], 'rw_r_k': rw_r_k[l],
            'rw_gn_w': rw_gn_w[l], 'rw_gn_b': rw_gn_b[l],
            's5_a_re': s5_a_re[l], 's5_a_im': s5_a_im[l], 's5_log_dt': s5_log_dt[l],
            's5_b_re': s5_b_re[l], 's5_b_im': s5_b_im[l], 's5_c_re': s5_c_re[l],
            's5_c_im': s5_c_im[l], 's5_d': s5_d[l], 's5_w_glu': s5_w_glu[l],
            'moe_router': moe_router[l], 'moe_w1': moe_w1[l], 'moe_w3': moe_w3[l],
            'moe_w2': moe_w2[l],
        }
        y_p, (k_c, v_c, rw_c, s5_c) = trunk_layer(y_p, cond_ctx, lp, None)
        ks.append(k_c)
        vs.append(v_c)
        rws.append(rw_c)
        s5s.append(s5_c)
        ctx = (cache_na_k[:, l], cache_na_v[:, l], state_rwkv[:, l], state_s5[:, l])
        y_s, _ = trunk_layer(y_s, cond_lat, lp, ctx)
    new_na_k = jnp.stack(ks, axis=1)
    new_na_v = jnp.stack(vs, axis=1)
    new_rwkv = jnp.stack(rws, axis=1)
    new_s5 = jnp.stack(s5s, axis=1)
    return (y_p, y_s, new_na_k, new_na_v, new_rwkv, new_s5)
```

```python
import functools
import math

import numpy as np
import jax
import jax.numpy as jnp
from jax import lax
from jax.experimental import pallas as pl
from jax.experimental.pallas import tpu as pltpu

F32 = jnp.float32
BF16 = jnp.bfloat16

D_MODEL = 4096
BATCH = 16
SEQ = 256
DEPTH = 2
DEC_BATCH = 4
DEC_SEQ = 2048
PAST_LEN = 256
GRID_W = 64
GROUP_W = 1024
FT_BLOCKS = 4
FT_CH = 256
NA_HEADS = 16
NA_HD = 64
NA_KR = 8
NA_KC = 16
RW_N = 64
RW_HEADS = 16
RW_DECAY_LORA = 64
RW_A_LORA = 64
RW_G_LORA = 160
RW_COLS = 3 * GROUP_W + 2 * RW_DECAY_LORA + 2 * RW_A_LORA + RW_G_LORA
RW_DECAY_SCALE = math.exp(-0.5)
RW_GN_EPS = 64e-5
S5_IN = 16
S5_GROUPS = 64
S5_N = 64
N_EXPERTS = 16
EXPERT_FF = 2048
EC_CAPACITY = 2
LN_EPS = 1e-5
DN_ALPHA = (2 * DEPTH) ** 0.25
OFF_NA = GROUP_W
OFF_RW = OFF_NA + 3 * GROUP_W
OFF_S5 = OFF_RW + RW_COLS
IN_COLS = OFF_S5 + GROUP_W

N_CTX = BATCH * SEQ
N_LAT = DEC_BATCH * DEC_SEQ
N_TOK = N_CTX + N_LAT
LANES = 128
OFF_S5_PAD = 7680
IN_COLS_PAD = OFF_S5_PAD + GROUP_W
VMEM_LIMIT = 56 * 1024 * 1024
NEG_BIG = -1e30
RW_CHUNK = 64
RW_PAIRS = 4
S5_TC = 64
S5_SEQ = 8


def _cp(sem):
    return pltpu.CompilerParams(dimension_semantics=sem, vmem_limit_bytes=VMEM_LIMIT)


def _dot(a, b):
    return jnp.dot(a, b, preferred_element_type=F32)


def _dot_nt(a, b):
    return lax.dot_general(a, b, (((1,), (1,)), ((), ())), preferred_element_type=F32)


def _dot_tn(a, b):
    return lax.dot_general(a, b, (((0,), (0,)), ((), ())), preferred_element_type=F32)


def _mm_body(x_ref, w_ref, o_ref):
    o_ref[...] = _dot(x_ref[...].astype(BF16), w_ref[...].astype(BF16)).astype(o_ref.dtype)


def matmul(x, w, *, tm, tn, out_dtype=F32):
    squeeze = x.ndim == 2
    if squeeze:
        x, w = x[None], w[None]
    gx, m, k = x.shape
    g, _, n = w.shape
    assert m % tm == 0 and n % tn == 0, (x.shape, w.shape, tm, tn)
    x_map = (lambda gi, i, j: (gi, i, 0)) if gx == g else (lambda gi, i, j: (0, i, 0))
    out = pl.pallas_call(
        _mm_body,
        grid=(g, m // tm, n // tn),
        in_specs=[pl.BlockSpec((None, tm, k), x_map),
                  pl.BlockSpec((None, k, tn), lambda gi, i, j: (gi, 0, j))],
        out_specs=pl.BlockSpec((None, tm, tn), lambda gi, i, j: (gi, i, j)),
        out_shape=jax.ShapeDtypeStruct((g, m, n), out_dtype),
        compiler_params=_cp(("parallel", "parallel", "arbitrary")),
    )(x, w)
    return out[0] if squeeze else out


def _mm3_body(x_ref, w_ref, o_ref):
    x = x_ref[...]
    w = w_ref[...]
    xh = x.astype(BF16)
    xl = (x - xh.astype(F32)).astype(BF16)
    wh = w.astype(BF16)
    wl = (w - wh.astype(F32)).astype(BF16)
    o_ref[...] = _dot(xh, wh) + _dot(xl, wh) + _dot(xh, wl)


def matmul_split(x, w, *, tm):
    m, k = x.shape
    n = w.shape[1]
    return pl.pallas_call(
        _mm3_body,
        grid=(m // tm,),
        in_specs=[pl.BlockSpec((tm, k), lambda i: (i, 0)), pl.BlockSpec((k, n), lambda i: (0, 0))],
        out_specs=pl.BlockSpec((tm, n), lambda i: (i, 0)),
        out_shape=jax.ShapeDtypeStruct((m, n), F32),
        compiler_params=_cp(("parallel",)),
    )(x, w)


def _moe_up_body(x_ref, w1_ref, w3_ref, o_ref):
    x = x_ref[...]
    a = _dot(x, w1_ref[...].astype(BF16))
    b = _dot(x, w3_ref[...].astype(BF16))
    o_ref[...] = (a * jax.nn.sigmoid(a) * b).astype(o_ref.dtype)


def moe_up(xs, w1, w3, *, tn=256):
    e, r, d = xs.shape
    f = w1.shape[2]
    return pl.pallas_call(
        _moe_up_body,
        grid=(e, f // tn),
        in_specs=[pl.BlockSpec((None, r, d), lambda ei, j: (ei, 0, 0)),
                  pl.BlockSpec((None, d, tn), lambda ei, j: (ei, 0, j)),
                  pl.BlockSpec((None, d, tn), lambda ei, j: (ei, 0, j))],
        out_specs=pl.BlockSpec((None, r, tn), lambda ei, j: (ei, 0, j)),
        out_shape=jax.ShapeDtypeStruct((e, r, f), BF16),
        compiler_params=_cp(("parallel", "arbitrary")),
    )(xs, w1, w3)


def _moe_down_body(h_ref, w2_ref, g_ref, o_ref):
    o_ref[...] = _dot(h_ref[...], w2_ref[...].astype(BF16)) * g_ref[...]


def moe_down(h, w2, gate, *, tn=512):
    e, r, f = h.shape
    d = w2.shape[2]
    return pl.pallas_call(
        _moe_down_body,
        grid=(e, d // tn),
        in_specs=[pl.BlockSpec((None, r, f), lambda ei, j: (ei, 0, 0)),
                  pl.BlockSpec((None, f, tn), lambda ei, j: (ei, 0, j)),
                  pl.BlockSpec((None, r, 1), lambda ei, j: (ei, 0, 0))],
        out_specs=pl.BlockSpec((None, r, tn), lambda ei, j: (ei, 0, j)),
        out_shape=jax.ShapeDtypeStruct((e, r, d), F32),
        compiler_params=_cp(("parallel", "arbitrary")),
    )(h, w2, gate)


def _fourier_body(u_ref, dftc_ref, ctm_ref, ftw_ref, o_ref, ucs_ref, *, t_len, rows):
    @pl.when(pl.program_id(1) == 0)
    def _():
        for g in range(FT_BLOCKS):
            for r0 in range(0, t_len, rows):
                ug = u_ref[r0:r0 + rows, g * FT_CH:(g + 1) * FT_CH].astype(BF16)
                res = _dot(ug, dftc_ref[...])
                ucs_ref[r0:r0 + rows, g * FT_CH:(g + 1) * FT_CH] = res[:, :FT_CH].astype(BF16)
                ucs_ref[t_len + r0:t_len + r0 + rows, g * FT_CH:(g + 1) * FT_CH] = res[:, FT_CH:].astype(BF16)

    f = _dot(ctm_ref[...], ucs_ref[...])
    o_ref[...] = _dot(f.astype(BF16), ftw_ref[...])


def _dft_mats(t_len):
    def cos_sin(n):
        j = lax.iota(jnp.int32, n)
        ang = ((j[:, None] * j[None, :]) % n).astype(F32) * (2.0 * math.pi / n)
        return jnp.cos(ang) * (n ** -0.5), jnp.sin(ang) * (n ** -0.5)
    ct, st = cos_sin(t_len)
    cc, sc = cos_sin(FT_CH)
    return jnp.concatenate([ct, -st], 1).astype(BF16), jnp.concatenate([cc, sc], 1).astype(BF16)


def fourier(z, ftw, *, t_len, n_batch, row_block0, tq):
    ctm, dftc = _dft_mats(t_len)
    rows = min(t_len, 512)
    return pl.pallas_call(
        functools.partial(_fourier_body, t_len=t_len, rows=rows),
        grid=(n_batch, t_len // tq),
        in_specs=[pl.BlockSpec((t_len, GROUP_W), lambda b, q: (row_block0 + b, 0)),
                  pl.BlockSpec((FT_CH, 2 * FT_CH), lambda b, q: (0, 0)),
                  pl.BlockSpec((tq, 2 * t_len), lambda b, q: (q, 0)),
                  pl.BlockSpec((GROUP_W, GROUP_W), lambda b, q: (0, 0))],
        out_specs=pl.BlockSpec((tq, GROUP_W), lambda b, q: (b * (t_len // tq) + q, 0)),
        out_shape=jax.ShapeDtypeStruct((n_batch * t_len, GROUP_W), F32),
        scratch_shapes=[pltpu.VMEM((2 * t_len, GROUP_W), BF16)],
        compiler_params=_cp(("parallel", "arbitrary")),
    )(z, dftc, ctm, ftw)


def _lane_masks():
    lane = lax.broadcasted_iota(jnp.int32, (1, LANES), 1)
    m0 = (lane < NA_HD).astype(F32)
    return m0, 1.0 - m0


def _ctx_attn_body(q_ref, k_ref, v_ref, o_ref):
    m0, m1 = _lane_masks()
    q = q_ref[...]
    qs = jnp.concatenate([q * m0, q * m1], 0).astype(BF16)
    s = _dot_nt(qs, k_ref[...].astype(BF16)) * (NA_HD ** -0.5)
    s = s - jnp.max(s, -1, keepdims=True)
    p = jnp.exp(s)
    p = p / jnp.sum(p, -1, keepdims=True)
    o = _dot(p.astype(BF16), v_ref[...].astype(BF16))
    o_ref[...] = o[:SEQ] * m0 + o[SEQ:] * m1


def ctx_attention(z):
    qb = OFF_NA // LANES
    return pl.pallas_call(
        _ctx_attn_body,
        grid=(BATCH, NA_HEADS // 2),
        in_specs=[pl.BlockSpec((SEQ, LANES), lambda b, p: (b, qb + p)),
                  pl.BlockSpec((SEQ, LANES), lambda b, p: (b, qb + 8 + p)),
                  pl.BlockSpec((SEQ, LANES), lambda b, p: (b, qb + 16 + p))],
        out_specs=pl.BlockSpec((SEQ, LANES), lambda b, p: (b, p)),
        out_shape=jax.ShapeDtypeStruct((N_CTX, GROUP_W), F32),
        compiler_params=_cp(("parallel", "parallel")),
    )(z, z, z)


def _na_body(q_ref, k_ref, v_ref, kc_ref, vc_ref, bias_ref, o_ref):
    m0, m1 = _lane_masks()
    r = pl.program_id(2)
    rs = jnp.clip(r - NA_KR // 2, 0, DEC_SEQ // GRID_W - NA_KR)
    start = pl.multiple_of(rs * GRID_W, GRID_W)
    q = q_ref[...]
    qs = jnp.concatenate([q * m0, q * m1], 0).astype(BF16)
    kl = k_ref[pl.ds(start, NA_KR * GRID_W), :].astype(BF16)
    vl = v_ref[pl.ds(start, NA_KR * GRID_W), :].astype(BF16)
    scale = NA_HD ** -0.5
    s_loc = _dot_nt(qs, kl) * scale + bias_ref[...].reshape(2 * GRID_W, NA_KR * GRID_W)
    s_ctx = _dot_nt(qs, kc_ref[...].astype(BF16)) * scale
    m = jnp.maximum(jnp.max(s_loc, -1, keepdims=True), jnp.max(s_ctx, -1, keepdims=True))
    p_loc = jnp.exp(s_loc - m)
    p_ctx = jnp.exp(s_ctx - m)
    inv = 1.0 / (jnp.sum(p_loc, -1, keepdims=True) + jnp.sum(p_ctx, -1, keepdims=True))
    o = (_dot((p_loc * inv).astype(BF16), vl) + _dot((p_ctx * inv).astype(BF16), vc_ref[...].astype(BF16)))
    o_ref[...] = o[:GRID_W] * m0 + o[GRID_W:] * m1


def _na_bias_table(rpb):
    var = np.arange(NA_KR)[:, None]
    dr = np.arange(NA_KR)[None, :] + (NA_KR - 1) - var
    cols = np.arange(GRID_W)
    c0 = np.clip(cols - NA_KC // 2, 0, GRID_W - NA_KC)
    cc = cols[None, :]
    valid = (cc >= c0[:, None]) & (cc < c0[:, None] + NA_KC)
    dc = np.clip(cc - cols[:, None] + NA_KC - 1, 0, 2 * NA_KC - 2)
    tab = rpb[:, dr[:, :, None, None], dc[None, None]]
    tab = jnp.where(valid[None, None, None], tab.astype(F32), NEG_BIG)
    tab = jnp.transpose(tab, (1, 0, 3, 2, 4))
    return tab.reshape(NA_KR, NA_HEADS, GRID_W, NA_KR * GRID_W)


def nbr_attention(z, k_ctx, v_ctx, rpb):
    rows = DEC_SEQ // GRID_W
    qb = OFF_NA // LANES
    lat_q0 = N_CTX // GRID_W
    lat_b0 = N_CTX // DEC_SEQ
    bias = _na_bias_table(rpb)

    def var_of(r):
        return r - jnp.clip(r - NA_KR // 2, 0, rows - NA_KR)

    return pl.pallas_call(
        _na_body,
        grid=(DEC_BATCH, NA_HEADS // 2, rows),
        in_specs=[pl.BlockSpec((GRID_W, LANES), lambda b, p, r: (lat_q0 + b * rows + r, qb + p)),
                  pl.BlockSpec((DEC_SEQ, LANES), lambda b, p, r: (lat_b0 + b, qb + 8 + p)),
                  pl.BlockSpec((DEC_SEQ, LANES), lambda b, p, r: (lat_b0 + b, qb + 16 + p)),
                  pl.BlockSpec((None, PAST_LEN, LANES), lambda b, p, r: (b, 0, p)),
                  pl.BlockSpec((None, PAST_LEN, LANES), lambda b, p, r: (b, 0, p)),
                  pl.BlockSpec((None, 2, GRID_W, NA_KR * GRID_W), lambda b, p, r: (var_of(r), p, 0, 0))],
        out_specs=pl.BlockSpec((GRID_W, LANES), lambda b, p, r: (b * rows + r, p)),
        out_shape=jax.ShapeDtypeStruct((N_LAT, GROUP_W), F32),
        compiler_params=_cp(("parallel", "parallel", "arbitrary")),
    )(z, z, z, k_ctx, v_ctx, bias)


def _rwkv_body(r_ref, lw_ref, k_ref, v_ref, kk_ref, a_ref, s0_ref, y_ref, sfin_ref, s_ref):
    c = pl.program_id(2)
    cl = RW_CHUNK

    @pl.when(c == 0)
    def _():
        s_ref[...] = s0_ref[...]

    m0, m1 = _lane_masks()
    row = lax.broadcasted_iota(jnp.int32, (cl, 2 * cl), 0)
    sidx = lax.broadcasted_iota(jnp.int32, (cl, 2 * cl), 1) & (cl - 1)
    strict = sidx < row
    incl = sidx <= row
    eye_cc = (sidx == row).astype(F32)
    tri = (lax.broadcasted_iota(jnp.int32, (cl, cl), 1) <= lax.broadcasted_iota(jnp.int32, (cl, cl), 0)).astype(BF16)
    bd_rows = lax.broadcasted_iota(jnp.int32, (LANES, LANES), 0) < NA_HD
    bd_cols = lax.broadcasted_iota(jnp.int32, (LANES, LANES), 1) < NA_HD
    bd_mask = (bd_rows == bd_cols).astype(F32)

    def bd(x):
        return jnp.concatenate([x * m0, x * m1], 0)

    lw = lw_ref[...]
    l_hi = lw.astype(BF16)
    rem = lw - l_hi.astype(F32)
    l_mid = rem.astype(BF16)
    l_lo = (rem - l_mid.astype(F32)).astype(BF16)
    cs = _dot(tri, l_hi) + _dot(tri, l_mid) + _dot(tri, l_lo)
    tot = cs[cl - 1:cl, :]
    e_neg = jnp.exp(-cs)
    e_end = jnp.exp(tot - cs)
    kk = kk_ref[...]
    kv = k_ref[...]
    b = kk * a_ref[...]
    alpha = kk * jnp.exp(cs - lw)
    rho = r_ref[...] * jnp.exp(cs)
    kap = kv * e_neg
    bet = b * e_neg
    kap_end = kv * e_end
    bet_end = b * e_end
    w_end = jnp.exp(tot)
    vv = v_ref[...]

    for p in range(RW_PAIRS):
        sl = slice(p * LANES, (p + 1) * LANES)
        s_prev = s_ref[p]
        ap = jnp.concatenate([alpha[:, sl], rho[:, sl]], 0).astype(BF16)
        kp, bp = kap[:, sl], bet[:, sl]
        zm = jnp.concatenate([kp * m0, kp * m1, bp * m0, bp * m1], 0).astype(BF16)
        g = _dot_nt(ap, zm)
        l_kap = jnp.where(strict, g[:cl, :2 * cl], 0.0)
        l_bet = jnp.where(strict, g[:cl, 2 * cl:], 0.0)
        m_kap = jnp.where(incl, g[cl:, :2 * cl], 0.0)
        m_bet = jnp.where(incl, g[cl:, 2 * cl:], 0.0)
        a_s = _dot_nt(ap, s_prev.astype(BF16))
        x = -l_bet
        t_inv = eye_cc + x
        for _ in range(5):
            x = _dot(x.astype(BF16), bd(x).astype(BF16))
            t_inv = t_inv + _dot(t_inv.astype(BF16), bd(x).astype(BF16))
        vp = vv[:, sl]
        vm = bd(vp).astype(BF16)
        rhs = a_s[:cl] + _dot(l_kap.astype(BF16), vm)
        u = -_dot(t_inv.astype(BF16), bd(rhs).astype(BF16))
        um = bd(u).astype(BF16)
        y = a_s[cl:] + _dot(jnp.concatenate([m_kap, m_bet], 1).astype(BF16), jnp.concatenate([vm, um], 0))
        y_ref[:, sl] = y
        vu = jnp.concatenate([vp, u], 0).astype(BF16)
        kb = jnp.concatenate([kap_end[:, sl], bet_end[:, sl]], 0).astype(BF16)
        s_new = s_prev * w_end[:, sl] + _dot_tn(vu, kb) * bd_mask
        s_ref[p] = s_new

    @pl.when(c == pl.num_programs(2) - 1)
    def _():
        sfin_ref[...] = s_ref[...]


def rwkv_scan(r, lw, k, v, kk, a, s0):
    n_seq, t_len, _ = r.shape
    wl = RW_PAIRS * LANES
    npg = (RW_HEADS // 2) // RW_PAIRS
    seq_spec = pl.BlockSpec((None, RW_CHUNK, wl), lambda s, g, c: (s, c, g))
    st_spec = pl.BlockSpec((None, RW_PAIRS, LANES, LANES), lambda s, g, c: (s, g, 0, 0))
    return pl.pallas_call(
        _rwkv_body,
        grid=(n_seq, npg, t_len // RW_CHUNK),
        in_specs=[seq_spec] * 6 + [st_spec],
        out_specs=[seq_spec, st_spec],
        out_shape=[jax.ShapeDtypeStruct((n_seq, t_len, GROUP_W), F32),
                   jax.ShapeDtypeStruct((n_seq, RW_HEADS // 2, LANES, LANES), F32)],
        scratch_shapes=[pltpu.VMEM((RW_PAIRS, LANES, LANES), F32)],
        compiler_params=_cp(("parallel", "parallel", "arbitrary")),
    )(r, lw, k, v, kk, a, s0)


def _s5_body(u_ref, wb_ref, wcr_ref, wci_ref, lr_ref, li_ref, h0r_ref, h0i_ref,
             y_ref, hfr_ref, hfi_ref, hr_ref, hi_ref):
    c = pl.program_id(1)
    n_rows = S5_TC * S5_SEQ
    half = S5_GROUPS * S5_N // 8

    @pl.when(c == 0)
    def _():
        hr_ref[0:S5_SEQ, :] = h0r_ref[...]
        hi_ref[0:S5_SEQ, :] = h0i_ref[...]

    fwd_row = (lax.broadcasted_iota(jnp.int32, (n_rows, 1), 0) & (S5_SEQ - 1)) < S5_SEQ // 2
    u = u_ref[...].astype(BF16)
    for j in range(8):
        x = u[:, j * LANES:(j + 1) * LANES]
        pj = jnp.where(fwd_row, _dot(x, wb_ref[0, j]), _dot(x, wb_ref[1, j]))
        hr_ref[S5_SEQ:, j * half:(j + 1) * half] = pj[:, :half]
        hi_ref[S5_SEQ:, j * half:(j + 1) * half] = pj[:, half:]

    lam_r = lr_ref[...]
    lam_i = li_ref[...]

    def step(t, carry):
        o0 = pl.multiple_of(t * S5_SEQ, S5_SEQ)
        o1 = pl.multiple_of(t * S5_SEQ + S5_SEQ, S5_SEQ)
        pr = hr_ref[pl.ds(o0, S5_SEQ), :]
        pi = hi_ref[pl.ds(o0, S5_SEQ), :]
        hr_ref[pl.ds(o1, S5_SEQ), :] = lam_r * pr - lam_i * pi + hr_ref[pl.ds(o1, S5_SEQ), :]
        hi_ref[pl.ds(o1, S5_SEQ), :] = lam_r * pi + lam_i * pr + hi_ref[pl.ds(o1, S5_SEQ), :]
        return carry

    lax.fori_loop(0, S5_TC, step, 0)

    for j in range(8):
        h_r = hr_ref[S5_SEQ:, j * half:(j + 1) * half].astype(BF16)
        h_i = hi_ref[S5_SEQ:, j * half:(j + 1) * half].astype(BF16)
        y0 = _dot(h_r, wcr_ref[0, j]) + _dot(h_i, wci_ref[0, j])
        y1 = _dot(h_r, wcr_ref[1, j]) + _dot(h_i, wci_ref[1, j])
        y_ref[:, j * LANES:(j + 1) * LANES] = jnp.where(fwd_row, y0, y1)

    last_r = hr_ref[n_rows:n_rows + S5_SEQ, :]
    last_i = hi_ref[n_rows:n_rows + S5_SEQ, :]
    hr_ref[0:S5_SEQ, :] = last_r
    hi_ref[0:S5_SEQ, :] = last_i

    @pl.when(c == pl.num_programs(1) - 1)
    def _():
        hfr_ref[...] = last_r
        hfi_ref[...] = last_i


def s5_scan(u, wb, wcr, wci, lam_r, lam_i, h0r, h0i):
    n_grp, rows, _ = u.shape
    n_state = S5_GROUPS * S5_N
    blk = S5_TC * S5_SEQ
    full = lambda shape: pl.BlockSpec(shape, lambda g, c: (0,) * len(shape))
    st = pl.BlockSpec((None, S5_SEQ, n_state), lambda g, c: (g, 0, 0))
    return pl.pallas_call(
        _s5_body,
        grid=(n_grp, rows // blk),
        in_specs=[pl.BlockSpec((None, blk, GROUP_W), lambda g, c: (g, c, 0)),
                  full(wb.shape), full(wcr.shape), full(wci.shape),
                  full(lam_r.shape), full(lam_i.shape), st, st],
        out_specs=[pl.BlockSpec((None, blk, GROUP_W), lambda g, c: (g, c, 0)), st, st],
        out_shape=[jax.ShapeDtypeStruct((n_grp, rows, GROUP_W), F32),
                   jax.ShapeDtypeStruct((n_grp, S5_SEQ, n_state), F32),
                   jax.ShapeDtypeStruct((n_grp, S5_SEQ, n_state), F32)],
        scratch_shapes=[pltpu.VMEM((blk + S5_SEQ, n_state), F32),
                        pltpu.VMEM((blk + S5_SEQ, n_state), F32)],
        compiler_params=_cp(("parallel", "arbitrary")),
    )(u, wb, wcr, wci, lam_r, lam_i, h0r, h0i)


def _ln(x):
    mu = jnp.mean(x, -1, keepdims=True)
    var = jnp.mean(jnp.square(x - mu), -1, keepdims=True)
    return (x - mu) * lax.rsqrt(var + LN_EPS)


def _rows_ctx_lat(ctx_val, lat_val):
    w = ctx_val.shape[-1]
    return jnp.concatenate([jnp.broadcast_to(ctx_val, (N_CTX, w)),
                            jnp.repeat(lat_val, DEC_SEQ, axis=0)], 0)


def _seq_edges():
    t = np.concatenate([np.arange(N_CTX) % SEQ, np.arange(N_LAT) % DEC_SEQ])
    tl = np.concatenate([np.full(N_CTX, SEQ), np.full(N_LAT, DEC_SEQ)])
    first = (t == 0)[:, None]
    last = (t == tl - 1)[:, None]
    return jnp.asarray(~first, F32), jnp.asarray(~last, F32)


def _block_diag2(w):
    a, b = w.shape[1:]
    zero = jnp.zeros((a, b), w.dtype)
    return jnp.concatenate([jnp.concatenate([w[0], zero], 1), jnp.concatenate([zero, w[1]], 1)], 0)


def _both_dirs(x_fwd, x_bwd, n_b, t_len):
    f = x_fwd.reshape(n_b, t_len, -1)
    bw = jnp.flip(x_bwd.reshape(n_b, t_len, -1), axis=1)
    return jnp.concatenate([f, bw], 0)


def _pair_states(s):
    n = s.shape[0]
    s = s.reshape(n, RW_HEADS // 2, 2, RW_N, RW_N)
    zero = jnp.zeros_like(s[:, :, 0])
    top = jnp.concatenate([s[:, :, 0], zero], -1)
    bot = jnp.concatenate([zero, s[:, :, 1]], -1)
    return jnp.concatenate([top, bot], -2)


def _unpair_states(s):
    n = s.shape[0]
    return jnp.stack([s[:, :, :RW_N, :RW_N], s[:, :, RW_N:, RW_N:]], 2).reshape(n, RW_HEADS, RW_N, RW_N)


def _rwkv_mixer(z, lp, rw_state):
    g_w = GROUP_W
    not_first, not_last = _seq_edges()
    zr = z[:, OFF_RW:OFF_S5]
    conv = lp['rw_conv']
    prev = jnp.concatenate([jnp.zeros_like(zr[:1]), zr[:-1]], 0) * not_first
    nxt = jnp.concatenate([zr[1:], jnp.zeros_like(zr[:1])], 0) * not_last
    zf = conv[0] * prev + conv[1] * zr + conv[2] * nxt
    r, k, v = zf[:, :g_w], zf[:, g_w:2 * g_w], zf[:, 2 * g_w:3 * g_w]
    o1 = 3 * g_w + 2 * RW_DECAY_LORA
    o2 = o1 + 2 * RW_A_LORA
    wd, ad, gd = zf[:, 3 * g_w:o1], zf[:, o1:o2], zf[:, o2:]
    d_log = lp['rw_w0'].reshape(1, 2 * g_w) + matmul(jnp.tanh(wd), _block_diag2(lp['rw_w_up']), tm=1024, tn=1024)
    lw = -RW_DECAY_SCALE * jax.nn.sigmoid(d_log)
    a = jax.nn.sigmoid(lp['rw_a0'].reshape(1, 2 * g_w) + matmul(ad, _block_diag2(lp['rw_a_up']), tm=1024, tn=1024))
    g = matmul(jax.nn.sigmoid(gd), lp['rw_g_up'], tm=1024, tn=1024)
    kk = (k * lp['rw_k_k']).reshape(N_TOK, RW_HEADS, RW_N)
    kk = kk / jnp.maximum(jnp.sqrt(jnp.sum(kk * kk, -1, keepdims=True)), 1e-12)
    kk = kk.reshape(N_TOK, g_w)
    k_a = lp['rw_k_a']
    kd0 = k * (1 + (a[:, :g_w] - 1) * k_a)
    kd1 = k * (1 + (a[:, g_w:] - 1) * k_a)

    ys, fins = [], []
    for (lo, hi, n_b, t_len, s0) in ((0, N_CTX, BATCH, SEQ, None), (N_CTX, N_TOK, DEC_BATCH, DEC_SEQ, rw_state)):
        sl = slice(lo, hi)
        seqs = [_both_dirs(r[sl], r[sl], n_b, t_len),
                _both_dirs(lw[sl, :g_w], lw[sl, g_w:], n_b, t_len),
                _both_dirs(kd0[sl], kd1[sl], n_b, t_len),
                _both_dirs(v[sl], v[sl], n_b, t_len),
                _both_dirs(kk[sl], kk[sl], n_b, t_len),
                _both_dirs(a[sl, :g_w], a[sl, g_w:], n_b, t_len)]
        if s0 is None:
            s_init = jnp.zeros((2 * n_b, RW_HEADS // 2, LANES, LANES), F32)
        else:
            s_init = _pair_states(jnp.swapaxes(s0, 0, 1).reshape(2 * n_b, RW_HEADS, RW_N, RW_N))
        y, s_fin = rwkv_scan(*seqs, s_init)
        ys.append((y[:n_b] + jnp.flip(y[n_b:], axis=1)).reshape(n_b * t_len, g_w))
        fins.append(s_fin)
    y = jnp.concatenate(ys, 0).reshape(N_TOK, RW_HEADS, RW_N)
    mu = jnp.mean(y, -1, keepdims=True)
    var = jnp.mean(jnp.square(y - mu), -1, keepdims=True)
    yn = ((y - mu) * lax.rsqrt(var + RW_GN_EPS)).reshape(N_TOK, g_w) * lp['rw_gn_w'] + lp['rw_gn_b']
    rk = (r * (kd0 + kd1) * lp['rw_r_k'].reshape(1, g_w)).reshape(N_TOK, RW_HEADS, RW_N)
    bonus = (jnp.sum(rk, -1, keepdims=True) * v.reshape(N_TOK, RW_HEADS, RW_N)).reshape(N_TOK, g_w)
    out = (yn + bonus) * g
    ctx_fin = _unpair_states(fins[0]).reshape(2, BATCH, RW_HEADS, RW_N, RW_N)
    return out, jnp.swapaxes(ctx_fin, 0, 1)


def _s5_params(lp):
    lam = lax.complex(lp['s5_a_re'], lp['s5_a_im'])
    dt = jnp.exp(lp['s5_log_dt'])[..., None]
    lam_bar = jnp.exp(lam * dt)
    b_bar = ((lam_bar - 1) / lam)[..., None] * lax.complex(lp['s5_b_re'], lp['s5_b_im'])
    c_mat = lax.complex(lp['s5_c_re'], lp['s5_c_im'])
    eye8 = jnp.eye(8, dtype=F32)

    def in_proj(m):
        m = m.reshape(2, 8, 8, S5_N, S5_IN)
        return jnp.einsum('djgpi,gh->djgihp', m, eye8).reshape(2, 8, 8 * S5_IN, 8 * S5_N)

    def out_proj(m):
        m = m.reshape(2, 8, 8, S5_IN, S5_N)
        return jnp.einsum('djgip,gh->djgphi', m, eye8).reshape(2, 8, 8 * S5_N, 8 * S5_IN)

    wb = jnp.concatenate([in_proj(b_bar.real), in_proj(b_bar.imag)], -1).astype(BF16)
    wcr = out_proj(c_mat.real).astype(BF16)
    wci = out_proj(-c_mat.imag).astype(BF16)
    n_state = S5_GROUPS * S5_N

    def per_row(x):
        return jnp.repeat(x.reshape(2, n_state), S5_SEQ // 2, axis=0)

    return wb, wcr, wci, per_row(lam_bar.real), per_row(lam_bar.imag)


def _s5_mixer(z, lp, s5_state):
    g_w = GROUP_W
    n_state = S5_GROUPS * S5_N
    wb, wcr, wci, lam_r, lam_i = _s5_params(lp)
    u = z[:, OFF_S5_PAD:OFF_S5_PAD + g_w]
    ys, fins = [], []
    for (lo, hi, n_b, t_len, h0) in ((0, N_CTX, BATCH, SEQ, None), (N_CTX, N_TOK, DEC_BATCH, DEC_SEQ, s5_state)):
        n_grp = n_b // 4
        ub = u[lo:hi].reshape(n_grp, 4, t_len, g_w)
        both = jnp.stack([ub, jnp.flip(ub, axis=2)], 1)
        rows = jnp.transpose(both, (0, 3, 1, 2, 4)).reshape(n_grp, t_len * S5_SEQ, g_w)
        if h0 is None:
            h0r = jnp.zeros((n_grp, S5_SEQ, n_state), F32)
            h0i = h0r
        else:
            hh = jnp.swapaxes(h0, 0, 1).reshape(1, S5_SEQ, n_state, 2)
            h0r, h0i = hh[..., 0], hh[..., 1]
        y, hfr, hfi = s5_scan(rows, wb, wcr, wci, lam_r, lam_i, h0r, h0i)
        y = jnp.transpose(y.reshape(n_grp, t_len, 2, 4, g_w), (0, 2, 3, 1, 4))
        y = y[:, 0] + jnp.flip(y[:, 1], axis=2)
        ys.append(y.reshape(n_b * t_len, g_w))
        fins.append(jnp.stack([hfr, hfi], -1))
    y = jnp.concatenate(ys, 0) + u * lp['s5_d'].reshape(1, g_w)
    yg = jax.nn.gelu(y)
    out = yg * jax.nn.sigmoid(matmul(yg, lp['s5_w_glu'], tm=1024, tn=1024))
    fin = fins[0].reshape(BATCH // 4, 2, 4, S5_GROUPS, S5_N, 2)
    fin = jnp.transpose(fin, (0, 2, 1, 3, 4, 5)).reshape(BATCH, 2, S5_GROUPS, S5_N, 2)
    return out, fin


def _expert_ffn(u, lp):
    router = jnp.pad(lp['moe_router'], ((0, 0), (0, LANES - N_EXPERTS)))
    logits = matmul_split(u, router, tm=512)[:, :N_EXPERTS]
    aff = jax.nn.softmax(logits, -1)
    gates, idxs = [], []
    for lo, hi in ((0, N_CTX), (N_CTX, N_TOK)):
        cap = EC_CAPACITY * (hi - lo) // N_EXPERTS
        gate, idx = lax.top_k(aff[lo:hi].T, cap)
        gates.append(gate)
        idxs.append(idx + lo)
    gate = jnp.concatenate(gates, 1)
    idx = jnp.concatenate(idxs, 1)
    xs = u.astype(BF16)[idx]
    h = moe_up(xs, lp['moe_w1'], lp['moe_w3'])
    ye = moe_down(h, lp['moe_w2'], gate[..., None])
    return jnp.zeros_like(u).at[idx.reshape(-1)].add(ye.reshape(-1, D_MODEL))


def _layer(x, mods, lp, ctx):
    sh1, sc1, g1, sh2, sc2, g2 = mods
    k_ctx, v_ctx, rw_state, s5_state = ctx
    u1 = (_ln(x) * (1 + sc1) + sh1).astype(BF16)
    w_in = lp['w_in']
    w_in_p = jnp.concatenate([w_in[:, :OFF_S5], jnp.zeros((D_MODEL, OFF_S5_PAD - OFF_S5), F32),
                              w_in[:, OFF_S5:]], 1).astype(BF16)
    z = matmul(u1, w_in_p, tm=1024, tn=512)
    ftw = lp['ft_w'].astype(BF16)
    y_ft = jnp.concatenate([
        fourier(z, ftw, t_len=SEQ, n_batch=BATCH, row_block0=0, tq=SEQ),
        fourier(z, ftw, t_len=DEC_SEQ, n_batch=DEC_BATCH, row_block0=N_CTX // DEC_SEQ, tq=512)], 0)
    y_na = jnp.concatenate([
        ctx_attention(z),
        nbr_attention(z, k_ctx.reshape(DEC_BATCH, PAST_LEN, GROUP_W), v_ctx.reshape(DEC_BATCH, PAST_LEN, GROUP_W),
                      lp['na_rpb'])], 0)
    y_rw, rw_fin = _rwkv_mixer(z, lp, rw_state)
    y_s5, s5_fin = _s5_mixer(z, lp, s5_state)
    ycat = jnp.concatenate([y_ft, y_na, y_rw, y_s5], -1).astype(BF16)
    m = matmul(ycat, lp['w_out'], tm=1024, tn=512)
    x = _ln(DN_ALPHA * x + g1 * m) * lp['ln1_g'] + lp['ln1_b']
    u2 = _ln(x) * (1 + sc2) + sh2
    f = _expert_ffn(u2, lp)
    x = _ln(DN_ALPHA * x + g2 * f) * lp['ln2_g'] + lp['ln2_b']
    k_new = z[:N_CTX, OFF_NA + GROUP_W:OFF_NA + 2 * GROUP_W].reshape(BATCH, SEQ, NA_HEADS, NA_HD)
    v_new = z[:N_CTX, OFF_NA + 2 * GROUP_W:OFF_NA + 3 * GROUP_W].reshape(BATCH, SEQ, NA_HEADS, NA_HD)
    return x, (k_new, v_new, rw_fin, s5_fin)


def kernel(x_prompt, x_sample, c, cache_na_k, cache_na_v, state_rwkv, state_s5, c_ctx, w_ada, b_ada, w_in, w_out, ln1_g, ln1_b, ln2_g, ln2_b, ft_w, na_rpb, rw_conv, rw_w_up, rw_w0, rw_a_up, rw_a0, rw_g_up, rw_k_k, rw_k_a, rw_r_k, rw_gn_w, rw_gn_b, s5_a_re, s5_a_im, s5_log_dt, s5_b_re, s5_b_im, s5_c_re, s5_c_im, s5_d, s5_w_glu, moe_router, moe_w1, moe_w3, moe_w2):
    params = dict(w_in=w_in, w_out=w_out, ln1_g=ln1_g, ln1_b=ln1_b, ln2_g=ln2_g, ln2_b=ln2_b, ft_w=ft_w,
                  na_rpb=na_rpb, rw_conv=rw_conv, rw_w_up=rw_w_up, rw_w0=rw_w0, rw_a_up=rw_a_up, rw_a0=rw_a0,
                  rw_g_up=rw_g_up, rw_k_k=rw_k_k, rw_k_a=rw_k_a, rw_r_k=rw_r_k, rw_gn_w=rw_gn_w,
                  rw_gn_b=rw_gn_b, s5_a_re=s5_a_re, s5_a_im=s5_a_im, s5_log_dt=s5_log_dt, s5_b_re=s5_b_re,
                  s5_b_im=s5_b_im, s5_c_re=s5_c_re, s5_c_im=s5_c_im, s5_d=s5_d, s5_w_glu=s5_w_glu,
                  moe_router=moe_router, moe_w1=moe_w1, moe_w3=moe_w3, moe_w2=moe_w2)
    cond = jnp.concatenate([c_ctx[None], c, jnp.zeros((8 - 1 - DEC_BATCH, D_MODEL), F32)], 0)
    mods_all = matmul(jax.nn.silu(cond)[None], w_ada, tm=8, tn=512) + b_ada[:, None, :]
    x = jnp.concatenate([x_prompt.reshape(N_CTX, D_MODEL), x_sample.reshape(N_LAT, D_MODEL)], 0)
    ks, vs, rws, s5s = [], [], [], []
    for l in range(DEPTH):
        lp = {name: val[l] for name, val in params.items()}
        mods = [_rows_ctx_lat(mm[0:1], mm[1:1 + DEC_BATCH]) for mm in jnp.split(mods_all[l], 6, axis=-1)]
        ctx = (cache_na_k[:, l], cache_na_v[:, l], state_rwkv[:, l], state_s5[:, l])
        x, (k_c, v_c, rw_c, s5_c) = _layer(x, mods, lp, ctx)
        ks.append(k_c)
        vs.append(v_c)
        rws.append(rw_c)
        s5s.append(s5_c)
    y_p = x[:N_CTX].reshape(BATCH, SEQ, D_MODEL)
    y_s = x[N_CTX:].reshape(DEC_BATCH, DEC_SEQ, D_MODEL)
    return (y_p, y_s, jnp.stack(ks, 1), jnp.stack(vs, 1), jnp.stack(rws, 1), jnp.stack(s5s, 1))
```
